```python
import jax
import jax.numpy as jnp
from jax import lax
import numpy as np

D_MODEL = 2048
BATCH = 2
SEQ = 4096
DEPTH = 2
DEC_BATCH = 8
DEC_SEQ = 4
PAST_LEN = 16384
PAGE_SIZE = 128

A_WIDTH = D_MODEL // 2
A_GROUPS = 8
CHUNK = 128
HEAD_DIM = 128
N_HEADS = D_MODEL // HEAD_DIM
KV_HEADS = 4
IDX_HEADS = 8
IDX_DIM = 64
INDEX_TOPK = 256
Q_BLOCK = 128
ROPE_THETA = 10000.0
C_HEAD_DIM = 64
C_WIDTH = D_MODEL // 2
C_HEADS = C_WIDTH // C_HEAD_DIM
C_DECAY_RANK = 64
C_ICL_RANK = 64
C_GATE_RANK = 128
C_GN_EPS = 64e-5
N_EXPERTS = 32
TOP_K = 4
EXPERT_FF = D_MODEL
SWIGLU_LIMIT = 7.0
SWIGLU_ALPHA = 1.702
MOE_ROW_BLOCK = 256
LN_EPS = 1e-5
DEEPNORM_ALPHA = (2 * DEPTH) ** 0.25
DEEPNORM_BETA = (8 * DEPTH) ** -0.25
A_COLS = 2 * A_WIDTH
B_Q = N_HEADS * HEAD_DIM
B_KV = KV_HEADS * HEAD_DIM
B_COLS = B_Q + 2 * B_KV + IDX_HEADS * IDX_DIM + IDX_DIM + IDX_HEADS
C_COLS = 3 * C_WIDTH + C_DECAY_RANK + C_ICL_RANK + C_GATE_RANK
G_COLS = 3 * D_MODEL
PROJ_COLS = A_COLS + B_COLS + C_COLS + G_COLS

kernel_name = 'hybrid_gmlp_dsa_rwkv7_moe_step'


def split_cols(x, sizes):
    bounds, acc = [], 0
    for s in sizes[:-1]:
        acc += s
        bounds.append(acc)
    return jnp.split(x, bounds, axis=-1)


def layer_norm(x, g, b, eps=LN_EPS):
    xf = x.astype(jnp.float32)
    mu = jnp.mean(xf, -1, keepdims=True)
    var = jnp.mean(jnp.square(xf - mu), -1, keepdims=True)
    return ((xf - mu) * lax.rsqrt(var + eps) * g + b).astype(x.dtype)


def rope(x, pos):
    half = x.shape[-1] // 2
    inv = ROPE_THETA ** (-jnp.arange(half, dtype=jnp.float32) / half)
    ang = pos.astype(jnp.float32)[:, None] * inv[None, :]
    cos, sin = jnp.cos(ang)[None, :, None, :], jnp.sin(ang)[None, :, None, :]
    xf = x.astype(jnp.float32)
    x1, x2 = xf[..., :half], xf[..., half:]
    return jnp.concatenate([x1 * cos - x2 * sin, x2 * cos + x1 * sin], -1).astype(x.dtype)


def chunk_gmlp(uv, lp):
    B, T, _ = uv.shape
    u, v = jnp.split(jax.nn.gelu(uv, approximate=False), 2, axis=-1)
    v = layer_norm(v, lp['ln_v_g'], lp['ln_v_b'])
    c = min(T, CHUNK)
    n = -(-T // c)
    vp = jnp.pad(v, ((0, 0), (0, n * c - T), (0, 0))).reshape(B, n, c, A_GROUPS, A_WIDTH // A_GROUPS)
    w_s = jnp.tril(lp['sgu_w'][:, :c, :c])
    s = jnp.einsum('gij,bnjgd->bnigd', w_s, vp) + lp['sgu_b'][:, :c].T[None, None, :, :, None]
    s = s.reshape(B, n * c, A_WIDTH)[:, :T]
    return (u * s) @ lp['w_a_out'], v


def indexer_scores(qi, wi, ki):
    dots = jnp.einsum('bqhd,bsd->bqhs', qi.astype(jnp.float32), ki.astype(jnp.float32)) * IDX_DIM ** -0.5
    return jnp.einsum('bqh,bqhs->bqs', wi.astype(jnp.float32) * IDX_HEADS ** -0.5, jax.nn.relu(dots))


def select_topk(scores, q_pos, key_pos, topk):
    admissible = key_pos[None, None, :] <= q_pos[None, :, None]
    vals, idx = lax.top_k(jnp.where(admissible, scores, -jnp.inf), topk)
    return idx, jnp.isfinite(vals)


def attend_selected(q, k_sel, v_sel, valid):
    B, Tq = q.shape[:2]
    qg = q.reshape(B, Tq, KV_HEADS, N_HEADS // KV_HEADS, HEAD_DIM).astype(jnp.float32)
    logits = jnp.einsum('bqgrd,bqkgd->bqgrk', qg, k_sel.astype(jnp.float32)) * HEAD_DIM ** -0.5
    p = jax.nn.softmax(jnp.where(valid[:, :, None, None, :], logits, -jnp.inf), axis=-1)
    o = jnp.einsum('bqgrk,bqkgd->bqgrd', p, v_sel.astype(jnp.float32))
    return o.reshape(B, Tq, B_Q).astype(q.dtype)


def take_rows(rows, idx):
    return jax.vmap(lambda rb, ib: rb[ib])(rows, idx)


def dsa_prompt(q, k, v, qi, ki, wi, pos, topk):
    B, T = q.shape[:2]
    qb = min(Q_BLOCK, T)
    nb = T // qb

    def block(args):
        q_b, qi_b, wi_b, pos_b = args
        idx, valid = select_topk(indexer_scores(qi_b, wi_b, ki), pos_b, pos, topk)
        return attend_selected(q_b, take_rows(k, idx), take_rows(v, idx), valid)

    to_blocks = lambda a: jnp.moveaxis(a.reshape((B, nb, qb) + a.shape[2:]), 1, 0)
    out = lax.map(block, (to_blocks(q), to_blocks(qi), to_blocks(wi), pos.reshape(nb, qb)))
    return jnp.moveaxis(out, 0, 1).reshape(B, T, B_Q)


def dsa_sample(q, k, v, qi, ki, wi, pos, paged, topk):
    pool_k, pool_v, pool_ik, page_table = paged
    DB, T = q.shape[:2]
    ki_all = jnp.concatenate([pool_ik[page_table].reshape(DB, PAST_LEN, IDX_DIM), ki], axis=1)
    key_pos = jnp.arange(PAST_LEN + T, dtype=jnp.int32)
    idx, valid = select_topk(indexer_scores(qi, wi, ki_all), pos, key_pos, topk)
    in_past = (idx < PAST_LEN)[..., None, None]
    pidx = jnp.minimum(idx, PAST_LEN - 1)
    page = jnp.take_along_axis(page_table, (pidx // PAGE_SIZE).reshape(DB, -1), axis=1).reshape(idx.shape)
    off = pidx % PAGE_SIZE
    nidx = jnp.clip(idx - PAST_LEN, 0, T - 1)
    k_sel = jnp.where(in_past, pool_k[page, off], take_rows(k, nidx))
    v_sel = jnp.where(in_past, pool_v[page, off], take_rows(v, nidx))
    return attend_selected(q, k_sel, v_sel, valid)


def rwkv7_branch(cx, shift_prev, state0, lp):
    B, T, _ = cx.shape
    prev = jnp.concatenate([shift_prev[:, None, :].astype(cx.dtype), cx[:, :-1]], axis=1)
    xm = cx + (prev - cx) * lp['mu_shift']
    r, k, v, wd, ad, gd = split_cols(xm, [C_WIDTH, C_WIDTH, C_WIDTH, C_DECAY_RANK, C_ICL_RANK, C_GATE_RANK])
    w = -jax.nn.softplus(-(lp['w0'] + jnp.tanh(wd) @ lp['w_up'])) - 0.5
    a = jax.nn.sigmoid(lp['a0'] + ad @ lp['a_up'])
    g = jax.nn.sigmoid(gd) @ lp['g_up']
    kk = k * lp['k_k']
    k = k * (1 + (a - 1) * lp['k_a'])
    heads = lambda t: t.astype(jnp.float32).reshape(B, T, C_HEADS, C_HEAD_DIM)
    rh, kh, vh, ah = heads(r), heads(k), heads(v), heads(a)
    kk = heads(kk)
    kk = kk * lax.rsqrt(jnp.maximum(jnp.sum(kk * kk, -1, keepdims=True), 1e-24))
    decay = jnp.exp(-jnp.exp(heads(w)))

    def step(S, inp):
        r_t, d_t, k_t, v_t, kk_t, a_t = inp
        sa = jnp.einsum('bhij,bhj->bhi', S, -kk_t)
        S = S * d_t[:, :, None, :] + sa[..., None] * (kk_t * a_t)[:, :, None, :] + v_t[..., None] * k_t[:, :, None, :]
        return S, jnp.einsum('bhij,bhj->bhi', S, r_t)

    seq_first = lambda t: jnp.moveaxis(t, 1, 0)
    S_fin, y = lax.scan(step, state0.astype(jnp.float32), tuple(seq_first(t) for t in (rh, decay, kh, vh, kk, ah)))
    y = seq_first(y)
    mu = jnp.mean(y, -1, keepdims=True)
    var = jnp.mean(jnp.square(y - mu), -1, keepdims=True)
    y = ((y - mu) * lax.rsqrt(var + C_GN_EPS)).reshape(B, T, C_WIDTH) * lp['ln_x_g'] + lp['ln_x_b']
    bonus = (jnp.sum(rh * kh * lp['r_k'], -1, keepdims=True) * vh).reshape(B, T, C_WIDTH)
    out = ((y + bonus) * g).astype(cx.dtype) @ lp['w_c_out']
    return out, S_fin.astype(cx.dtype), cx[:, -1]


def token_mixer(x, pos, lp, shift_prev, state0, paged):
    B, T, _ = x.shape
    a_cols, b_cols, c_cols, g_cols = split_cols(x @ lp['w_in'], [A_COLS, B_COLS, C_COLS, G_COLS])
    a_out, v_rows = chunk_gmlp(a_cols, lp)
    q, k, v, qi, ki, wi = split_cols(b_cols, [B_Q, B_KV, B_KV, IDX_HEADS * IDX_DIM, IDX_DIM, IDX_HEADS])
    q = rope(q.reshape(B, T, N_HEADS, HEAD_DIM), pos)
    k = rope(k.reshape(B, T, KV_HEADS, HEAD_DIM), pos)
    v = v.reshape(B, T, KV_HEADS, HEAD_DIM)
    qi = rope(qi.reshape(B, T, IDX_HEADS, IDX_DIM), pos)
    ki = rope(ki[:, :, None, :], pos)[:, :, 0]
    if paged is None:
        b_att = dsa_prompt(q, k, v, qi, ki, wi, pos, min(INDEX_TOPK, T // 4))
    else:
        b_att = dsa_sample(q, k, v, qi, ki, wi, pos, paged, min(INDEX_TOPK, (PAST_LEN + T) // 4))
    b_out = b_att @ lp['w_b_out']
    c_out, S_new, shift_new = rwkv7_branch(c_cols, shift_prev, state0, lp)
    gates = jax.nn.sigmoid(g_cols.astype(jnp.float32)).reshape(B, T, 3, D_MODEL)
    merged = gates[:, :, 0] * a_out + gates[:, :, 1] * b_out + gates[:, :, 2] * c_out
    return merged.astype(x.dtype) @ lp['w_o'], (k, v, ki, S_new, shift_new, v_rows)


def moe(x, lp):
    B, T, D = x.shape
    xt = x.reshape(-1, D)
    N = xt.shape[0]
    logits = (xt @ lp['router_w'] + lp['router_b']).astype(jnp.float32)
    top_val, top_idx = lax.top_k(logits, TOP_K)
    gate = jax.nn.softmax(top_val, axis=-1)
    M, R = N * TOP_K, MOE_ROW_BLOCK
    nb = -(-M // R) + N_EXPERTS
    P = nb * R
    e_flat = top_idx.reshape(M)
    tok = jnp.arange(M, dtype=jnp.int32) // TOP_K
    g_flat = gate.reshape(M)
    order = jnp.argsort(e_flat)
    e_sorted = e_flat[order]
    counts = jnp.bincount(e_flat, length=N_EXPERTS)
    padded = (counts + R - 1) // R * R
    seg_end_pad = jnp.cumsum(padded)
    seg_start_pad = seg_end_pad - padded
    seg_start = jnp.cumsum(counts) - counts
    dest = seg_start_pad[e_sorted] + jnp.arange(M, dtype=jnp.int32) - seg_start[e_sorted]
    row_tok = jnp.full((P,), N, jnp.int32).at[dest].set(tok[order])
    row_gate = jnp.zeros((P,), jnp.float32).at[dest].set(g_flat[order])
    block_exp = jnp.minimum(jnp.searchsorted(seg_end_pad, jnp.arange(nb, dtype=jnp.int32) * R, side='right'), N_EXPERTS - 1)
    xs = jnp.concatenate([xt, jnp.zeros((1, D), xt.dtype)], axis=0)[row_tok].reshape(nb, R, D)

    def expert_block(args):
        xb, e = args
        h = xb @ lp['w_gu'][e] + lp['b_gu'][e]
        glu, lin = jnp.split(h, 2, axis=-1)
        glu = jnp.minimum(glu, SWIGLU_LIMIT)
        lin = jnp.clip(lin, -SWIGLU_LIMIT, SWIGLU_LIMIT)
        act = (lin + 1) * glu * jax.nn.sigmoid(SWIGLU_ALPHA * glu)
        return act @ lp['w_dn'][e] + lp['b_dn'][e]

    ys = lax.map(expert_block, (xs, block_exp)).reshape(P, D)
    out = jax.ops.segment_sum(ys * row_gate[:, None], row_tok, num_segments=N + 1)[:N]
    return out.reshape(B, T, D).astype(x.dtype)


def decoder_layer(x, pos, lp, shift_prev, state0, paged):
    mix, new_state = token_mixer(x, pos, lp, shift_prev, state0, paged)
    x = layer_norm(DEEPNORM_ALPHA * x + mix, lp['ln1_g'], lp['ln1_b'])
    x = layer_norm(DEEPNORM_ALPHA * x + moe(x, lp), lp['ln2_g'], lp['ln2_b'])
    return x, new_state


def setup_inputs(seed: int = 0) -> dict:
    key = jax.random.key(seed)
    ks = iter(jax.random.split(key, 48))
    nrm = lambda shape, scale: scale * jax.random.normal(next(ks), shape, jnp.float32)
    uni = lambda shape, lo, hi: jax.random.uniform(next(ks), shape, jnp.float32, lo, hi)
    L, beta = DEPTH, DEEPNORM_BETA
    n_pages = PAST_LEN // PAGE_SIZE
    n_pool = (DEC_BATCH * n_pages * 5) // 4
    page_table = jax.random.permutation(next(ks), n_pool)[:DEC_BATCH * n_pages].reshape(DEC_BATCH, n_pages).astype(jnp.int32)
    return {
        'x_prompt': nrm((BATCH, SEQ, D_MODEL), 1.0),
        'x_sample': nrm((DEC_BATCH, DEC_SEQ, D_MODEL), 1.0),
        'cache_k': nrm((L, n_pool, PAGE_SIZE, KV_HEADS, HEAD_DIM), 1.0),
        'cache_v': nrm((L, n_pool, PAGE_SIZE, KV_HEADS, HEAD_DIM), 1.0),
        'cache_idx_k': nrm((L, n_pool, PAGE_SIZE, IDX_DIM), 1.0),
        'state_rwkv': nrm((L, DEC_BATCH, C_HEADS, C_HEAD_DIM, C_HEAD_DIM), 1.0),
        'state_shift': nrm((L, DEC_BATCH, C_COLS), 1.0),
        'page_table': page_table,
        'w_in': nrm((L, D_MODEL, PROJ_COLS), D_MODEL ** -0.5),
        'ln_v_g': 1.0 + nrm((L, A_WIDTH), 0.02),
        'ln_v_b': nrm((L, A_WIDTH), 0.02),
        'sgu_w': nrm((L, A_GROUPS, CHUNK, CHUNK), CHUNK ** -0.5),
        'sgu_b': 1.0 + nrm((L, A_GROUPS, CHUNK), 0.02),
        'w_a_out': nrm((L, A_WIDTH, D_MODEL), beta * A_WIDTH ** -0.5),
        'w_b_out': nrm((L, B_Q, D_MODEL), beta * B_Q ** -0.5),
        'mu_shift': uni((L, C_COLS), 0.0, 1.0),
        'w0': uni((L, C_WIDTH), -4.0, 1.0),
        'w_up': nrm((L, C_DECAY_RANK, C_WIDTH), 0.1 * C_DECAY_RANK ** -0.5),
        'a0': nrm((L, C_WIDTH), 0.1),
        'a_up': nrm((L, C_ICL_RANK, C_WIDTH), 0.1 * C_ICL_RANK ** -0.5),
        'g_up': nrm((L, C_GATE_RANK, C_WIDTH), C_GATE_RANK ** -0.5),
        'k_k': 0.85 + nrm((L, C_WIDTH), 0.05),
        'k_a': 1.0 + nrm((L, C_WIDTH), 0.05),
        'r_k': nrm((L, C_HEADS, C_HEAD_DIM), 0.1),
        'ln_x_g': 1.0 + nrm((L, C_WIDTH), 0.02),
        'ln_x_b': nrm((L, C_WIDTH), 0.02),
        'w_c_out': nrm((L, C_WIDTH, D_MODEL), beta * C_WIDTH ** -0.5),
        'w_o': nrm((L, D_MODEL, D_MODEL), beta * D_MODEL ** -0.5),
        'ln1_g': 1.0 + nrm((L, D_MODEL), 0.02),
        'ln1_b': nrm((L, D_MODEL), 0.02),
        'router_w': nrm((L, D_MODEL, N_EXPERTS), D_MODEL ** -0.5),
        'router_b': nrm((L, N_EXPERTS), 0.01),
        'w_gu': nrm((L, N_EXPERTS, D_MODEL, 2 * EXPERT_FF), D_MODEL ** -0.5),
        'b_gu': nrm((L, N_EXPERTS, 2 * EXPERT_FF), 0.01),
        'w_dn': nrm((L, N_EXPERTS, EXPERT_FF, D_MODEL), beta * EXPERT_FF ** -0.5),
        'b_dn': nrm((L, N_EXPERTS, D_MODEL), 0.01),
        'ln2_g': 1.0 + nrm((L, D_MODEL), 0.02),
        'ln2_b': nrm((L, D_MODEL), 0.02),
    }


def reference(x_prompt, x_sample, cache_k, cache_v, cache_idx_k, state_rwkv, state_shift, page_table,
              w_in, ln_v_g, ln_v_b, sgu_w, sgu_b, w_a_out, w_b_out, mu_shift, w0, w_up, a0, a_up, g_up,
              k_k, k_a, r_k, ln_x_g, ln_x_b, w_c_out, w_o, ln1_g, ln1_b, router_w, router_b,
              w_gu, b_gu, w_dn, b_dn, ln2_g, ln2_b):
    params = dict(w_in=w_in, ln_v_g=ln_v_g, ln_v_b=ln_v_b, sgu_w=sgu_w, sgu_b=sgu_b, w_a_out=w_a_out,
                  w_b_out=w_b_out, mu_shift=mu_shift, w0=w0, w_up=w_up, a0=a0, a_up=a_up, g_up=g_up,
                  k_k=k_k, k_a=k_a, r_k=r_k, ln_x_g=ln_x_g, ln_x_b=ln_x_b, w_c_out=w_c_out, w_o=w_o,
                  ln1_g=ln1_g, ln1_b=ln1_b, router_w=router_w, router_b=router_b, w_gu=w_gu, b_gu=b_gu,
                  w_dn=w_dn, b_dn=b_dn, ln2_g=ln2_g, ln2_b=ln2_b)
    B, T = x_prompt.shape[:2]
    pos_p = jnp.arange(T, dtype=jnp.int32)
    pos_s = PAST_LEN + jnp.arange(x_sample.shape[1], dtype=jnp.int32)
    zero_shift = jnp.zeros((B, C_COLS), x_prompt.dtype)
    zero_state = jnp.zeros((B, C_HEADS, C_HEAD_DIM, C_HEAD_DIM), x_prompt.dtype)
    yp, ys = x_prompt, x_sample
    kp, vp, ikp, sp, shp = [], [], [], [], []
    ksl, vsl, iks, ss, shs, gvs = [], [], [], [], [], []
    for l in range(DEPTH):
        lp = {name: arr[l] for name, arr in params.items()}
        yp, (k1, v1, ik1, s1, sh1, _) = decoder_layer(yp, pos_p, lp, zero_shift, zero_state, None)
        ys, (k2, v2, ik2, s2, sh2, gv2) = decoder_layer(
            ys, pos_s, lp, state_shift[l], state_rwkv[l], (cache_k[l], cache_v[l], cache_idx_k[l], page_table))
        kp.append(k1); vp.append(v1); ikp.append(ik1); sp.append(s1); shp.append(sh1)
        ksl.append(k2); vsl.append(v2); iks.append(ik2); ss.append(s2); shs.append(sh2); gvs.append(gv2)
    return (yp, ys, jnp.stack(kp), jnp.stack(vp), jnp.stack(ikp), jnp.stack(sp), jnp.stack(shp),
            jnp.stack(ksl), jnp.stack(vsl), jnp.stack(iks), jnp.stack(ss), jnp.stack(shs), jnp.stack(gvs))
```

```python
import functools

import jax
import jax.numpy as jnp
from jax import lax
from jax.experimental import pallas as pl
from jax.experimental.pallas import tpu as pltpu

F32 = jnp.float32
BF16 = jnp.bfloat16
I32 = jnp.int32

D_MODEL = 2048
BATCH = 2
SEQ = 4096
DEPTH = 2
DEC_BATCH = 8
DEC_SEQ = 4
PAST_LEN = 16384
PAGE_SIZE = 128
N_PAGES = PAST_LEN // PAGE_SIZE

A_WIDTH = D_MODEL // 2
A_GROUPS = 8
CHUNK = 128
HEAD_DIM = 128
N_HEADS = D_MODEL // HEAD_DIM
KV_HEADS = 4
IDX_HEADS = 8
IDX_DIM = 64
INDEX_TOPK = 256
ROPE_THETA = 10000.0
C_HEAD_DIM = 64
C_WIDTH = D_MODEL // 2
C_HEADS = C_WIDTH // C_HEAD_DIM
C_DECAY_RANK = 64
C_ICL_RANK = 64
C_GATE_RANK = 128
C_GN_EPS = 64e-5
N_EXPERTS = 32
TOP_K = 4
EXPERT_FF = D_MODEL
SWIGLU_LIMIT = 7.0
SWIGLU_ALPHA = 1.702
LN_EPS = 1e-5
DEEPNORM_ALPHA = (2 * DEPTH) ** 0.25

A_COLS = 2 * A_WIDTH
B_Q = N_HEADS * HEAD_DIM
B_KV = KV_HEADS * HEAD_DIM
B_COLS = B_Q + 2 * B_KV + IDX_HEADS * IDX_DIM + IDX_DIM + IDX_HEADS
C_COLS = 3 * C_WIDTH + C_DECAY_RANK + C_ICL_RANK + C_GATE_RANK
G_COLS = 3 * D_MODEL

LANES = 128
SUBLANES = 8

NP_ROWS = BATCH * SEQ
NS_ROWS = DEC_BATCH * DEC_SEQ
ROW_TILE = 256
NT = NP_ROWS + ROW_TILE
QKV_W = 4096
KIWI_OFF = B_Q + 2 * B_KV + IDX_HEADS * IDX_DIM

MASK_NEG = -1e30
INT_MIN = -2 ** 31


def _cparams(sem, vmem_mb, **kw):
    return pltpu.CompilerParams(dimension_semantics=sem, vmem_limit_bytes=vmem_mb * 1024 * 1024, **kw)


def _layer_norm(x, g, b, eps):
    mu = jnp.mean(x, axis=-1, keepdims=True)
    xc = x - mu
    var = jnp.mean(xc * xc, axis=-1, keepdims=True)
    return xc * lax.rsqrt(var + eps) * g + b


def _gelu(x):
    return 0.5 * x * (1.0 + lax.erf(x * (2.0 ** -0.5)))


def _sigmoid(x):
    return 1.0 / (1.0 + jnp.exp(-x))


def _mm_kernel(x_ref, w_ref, o_ref, wbf_ref, *, act):
    @pl.when(pl.program_id(1) == 0)
    def _():
        wbf_ref[...] = w_ref[0].astype(BF16)

    acc = jnp.dot(x_ref[...], wbf_ref[...], preferred_element_type=F32)
    if act is not None:
        acc = act(acc)
    o_ref[...] = acc.astype(o_ref.dtype)


def _mm(x, w, l, *, n, tm, tn, col0=0, out_dtype=F32, act=None, name):
    m, k = x.shape
    assert m % tm == 0 and n % tn == 0 and col0 % tn == 0
    jb = col0 // tn
    return pl.pallas_call(
        functools.partial(_mm_kernel, act=act),
        grid=(n // tn, m // tm),
        in_specs=[pl.BlockSpec((tm, k), lambda j, i: (i, 0)), pl.BlockSpec((1, k, tn), lambda j, i: (l, 0, jb + j))],
        out_specs=pl.BlockSpec((tm, tn), lambda j, i: (i, j)),
        out_shape=jax.ShapeDtypeStruct((m, n), out_dtype),
        scratch_shapes=[pltpu.VMEM((k, tn), BF16)],
        compiler_params=_cparams(("arbitrary", "arbitrary"), 48),
        name=name,
    )(x, w)


def _gmlp_kernel(u_ref, v_ref, g_ref, b_ref, ws_ref, bias_ref, wa_ref, o_ref, vout_ref):
    i = pl.program_id(0)
    v = _layer_norm(v_ref[...], g_ref[...], b_ref[...], LN_EPS)

    @pl.when(i == NP_ROWS // ROW_TILE)
    def _():
        vout_ref[...] = v[:CHUNK]

    vb = v.astype(BF16)
    rows = []
    for c in range(ROW_TILE // CHUNK):
        parts = []
        for g in range(A_GROUPS):
            wg = ws_ref[0, g].astype(BF16)
            parts.append(jnp.dot(wg, vb[c * CHUNK:(c + 1) * CHUNK, g * LANES:(g + 1) * LANES],
                                 preferred_element_type=F32))
        rows.append(jnp.concatenate(parts, axis=1) + bias_ref[0])
    s = jnp.concatenate(rows, axis=0)
    us = (u_ref[...] * s).astype(BF16)
    o_ref[...] = jnp.dot(us, wa_ref[0], preferred_element_type=F32)


def _gmlp(uv, ln_g, ln_b, ws2, bias2, wa_bf, l):
    nt = uv.shape[0]
    sample_tile = NP_ROWS // ROW_TILE
    return pl.pallas_call(
        _gmlp_kernel,
        grid=(nt // ROW_TILE,),
        in_specs=[
            pl.BlockSpec((ROW_TILE, A_WIDTH), lambda i: (i, 0)),
            pl.BlockSpec((ROW_TILE, A_WIDTH), lambda i: (i, 1)),
            pl.BlockSpec((1, A_WIDTH), lambda i: (0, 0)),
            pl.BlockSpec((1, A_WIDTH), lambda i: (0, 0)),
            pl.BlockSpec((1, A_GROUPS, CHUNK, CHUNK), lambda i: (jnp.where(i >= sample_tile, 1, 0), 0, 0, 0)),
            pl.BlockSpec((1, CHUNK, A_WIDTH), lambda i: (jnp.where(i >= sample_tile, 1, 0), 0, 0)),
            pl.BlockSpec((1, A_WIDTH, D_MODEL), lambda i: (l, 0, 0)),
        ],
        out_specs=[
            pl.BlockSpec((ROW_TILE, D_MODEL), lambda i: (i, 0)),
            pl.BlockSpec((CHUNK, A_WIDTH), lambda i: (0, 0)),
        ],
        out_shape=[jax.ShapeDtypeStruct((nt, D_MODEL), F32), jax.ShapeDtypeStruct((CHUNK, A_WIDTH), F32)],
        compiler_params=_cparams(("arbitrary",), 48),
        name="gmlp",
    )(uv, uv, ln_g, ln_b, ws2, bias2, wa_bf)


def _rope128(x, c, s):
    return x * c + pltpu.roll(x, 64, 1) * s


def _rope64(x, c, sa, sb):
    return x * c + pltpu.roll(x, 96, 1) * sa + pltpu.roll(x, 32, 1) * sb


def _rope_kernel(x_ref, c128_ref, s128_ref, c64_ref, s64a_ref, s64b_ref,
                 q_ref, k_ref, kb_ref, vb_ref, qi_ref, kiwi_ref, ki2_ref):
    c = c128_ref[...]
    s = s128_ref[...]
    for h in range(N_HEADS):
        sl = slice(h * LANES, (h + 1) * LANES)
        q_ref[:, sl] = _rope128(x_ref[:, sl], c, s).astype(BF16)
    for h in range(KV_HEADS):
        kh = _rope128(x_ref[:, B_Q + h * LANES:B_Q + (h + 1) * LANES], c, s)
        k_ref[:, h * LANES:(h + 1) * LANES] = kh
        kb_ref[:, h * LANES:(h + 1) * LANES] = kh.astype(BF16)
    vb_ref[...] = x_ref[:, B_Q + B_KV:B_Q + 2 * B_KV].astype(BF16)
    c6 = c64_ref[...]
    sa = s64a_ref[...]
    sb = s64b_ref[...]
    qi0 = B_Q + 2 * B_KV
    for p in range(IDX_HEADS // 2):
        qi_ref[:, p * LANES:(p + 1) * LANES] = _rope64(x_ref[:, qi0 + p * LANES:qi0 + (p + 1) * LANES], c6, sa, sb)
    xk = x_ref[:, KIWI_OFF:KIWI_OFF + LANES]
    kr = _rope64(xk, c6, sa, sb)
    lane = lax.broadcasted_iota(I32, kr.shape, 1)
    kiwi_ref[...] = jnp.where(lane < IDX_DIM, kr, xk)
    ki2_ref[...] = jnp.where(lane < IDX_DIM, kr, pltpu.roll(kr, 64, 1)).astype(BF16)


def _rope(qkv, tabs):
    nt = qkv.shape[0]
    tm = ROW_TILE
    row = lambda w: pl.BlockSpec((tm, w), lambda i: (i, 0))
    return pl.pallas_call(
        _rope_kernel,
        grid=(nt // tm,),
        in_specs=[row(QKV_W)] + [row(LANES)] * 5,
        out_specs=[row(B_Q), row(B_KV), row(B_KV), row(B_KV), row(IDX_HEADS * IDX_DIM), row(LANES), row(LANES)],
        out_shape=[
            jax.ShapeDtypeStruct((nt, B_Q), BF16),
            jax.ShapeDtypeStruct((nt, B_KV), F32),
            jax.ShapeDtypeStruct((nt, B_KV), BF16),
            jax.ShapeDtypeStruct((nt, B_KV), BF16),
            jax.ShapeDtypeStruct((nt, IDX_HEADS * IDX_DIM), F32),
            jax.ShapeDtypeStruct((nt, LANES), F32),
            jax.ShapeDtypeStruct((nt, LANES), BF16),
        ],
        compiler_params=_cparams(("arbitrary",), 48),
        name="rope",
    )(qkv, *tabs)


KT = 256
QB = 128


def _sortable(x):
    bits = pltpu.bitcast(x, I32)
    return jnp.where(bits < 0, bits ^ jnp.int32(0x7FFFFFFF), bits)


def _dsa_prompt_kernel(q_ref, qi_ref, w_ref, tri_ref, ki2_ref, kb_ref, vb_ref, o_ref,
                       key_s, qm_s, qs_s, m_s, l_s, acc_s):
    n = pl.program_id(1)
    ntile = (n + 2) // 2
    lane_q = lax.broadcasted_iota(I32, (1, QB), 1)
    q_pos = n * QB + lane_q

    lanes = lax.broadcasted_iota(I32, (QB, LANES), 1)
    for p in range(IDX_HEADS // 2):
        xp = qi_ref[:, p * LANES:(p + 1) * LANES]
        qm_s[2 * p] = jnp.where(lanes < IDX_DIM, xp, 0.0).astype(BF16)
        qm_s[2 * p + 1] = jnp.where(lanes >= IDX_DIM, xp, 0.0).astype(BF16)
    w_t = jnp.transpose(w_ref[...])
    w_rows = [w_t[IDX_DIM + h:IDX_DIM + h + 1, :] * (IDX_HEADS ** -0.5) for h in range(IDX_HEADS)]

    def score_body(kt, carry):
        k0 = pl.multiple_of(kt * KT, KT)
        k_tile = ki2_ref[pl.ds(k0, KT), :]
        acc = jnp.zeros((KT, QB), F32)
        for h in range(IDX_HEADS):
            d = lax.dot_general(k_tile, qm_s[h], (((1,), (1,)), ((), ())), preferred_element_type=F32)
            acc = acc + jnp.maximum(d * (IDX_DIM ** -0.5), 0.0) * w_rows[h]
        kpos = k0 + lax.broadcasted_iota(I32, (KT, QB), 0)
        acc = jnp.where(kpos <= q_pos, acc, -jnp.inf)
        key_s[pl.ds(k0, KT), :] = _sortable(acc)
        return carry

    lax.fori_loop(0, ntile, score_body, 0)

    def count(pred):
        def body(kt, c8):
            k0 = pl.multiple_of(kt * KT, KT)
            hit = jnp.where(pred(key_s[pl.ds(k0, KT), :]), 1, 0).astype(I32)
            return c8 + hit.reshape(KT // SUBLANES, SUBLANES, QB).sum(axis=0)
        c8 = lax.fori_loop(0, ntile, body, jnp.zeros((SUBLANES, QB), I32))
        return c8.sum(axis=0, keepdims=True)

    def bit_body(it, thr):
        cand = thr + lax.shift_left(jnp.int32(1), 31 - it)
        cnt = count(lambda kk: kk >= cand)
        return jnp.where(cnt >= INDEX_TOPK, cand, thr)

    thr = lax.fori_loop(0, 32, bit_body, jnp.full((1, QB), INT_MIN, I32))
    need = (INDEX_TOPK - count(lambda kk: kk > thr)).astype(F32)

    for g in range(KV_HEADS):
        for r in range(N_HEADS // KV_HEADS):
            qs_s[g, r * QB:(r + 1) * QB, :] = q_ref[:, (g * 4 + r) * LANES:(g * 4 + r + 1) * LANES]
    m_s[...] = jnp.full(m_s.shape, MASK_NEG, F32)
    l_s[...] = jnp.zeros(l_s.shape, F32)
    acc_s[...] = jnp.zeros(acc_s.shape, F32)

    def att_body(kt, eq_seen):
        k0 = pl.multiple_of(kt * KT, KT)
        kk = key_s[pl.ds(k0, KT), :]
        eqf = jnp.where(kk == thr, 1.0, 0.0)
        rank = jnp.dot(tri_ref[...], eqf.astype(BF16), preferred_element_type=F32) + eq_seen
        tie = jnp.where(rank <= need, eqf, 0.0)
        sel = jnp.where(kk > thr, 1.0, tie)
        kpos = k0 + lax.broadcasted_iota(I32, (KT, QB), 0)
        bias = jnp.where(kpos <= q_pos, (sel - 1.0) * (-MASK_NEG), MASK_NEG)
        bias_t = jnp.concatenate([jnp.transpose(bias[c * QB:(c + 1) * QB, :]) for c in range(KT // QB)], axis=1)
        bias4 = jnp.concatenate([bias_t] * (N_HEADS // KV_HEADS), axis=0)
        for g in range(KV_HEADS):
            kg = kb_ref[pl.ds(k0, KT), g * LANES:(g + 1) * LANES]
            s = lax.dot_general(qs_s[g], kg, (((1,), (1,)), ((), ())), preferred_element_type=F32)
            s = s * (HEAD_DIM ** -0.5) + bias4
            m_prev = m_s[g]
            m_next = jnp.maximum(m_prev, jnp.max(s, axis=1, keepdims=True))
            alpha = jnp.exp(m_prev - m_next)
            p = jnp.exp(s - jnp.concatenate([m_next] * (KT // LANES), axis=1))
            l_s[g] = alpha * l_s[g] + jnp.sum(p, axis=1, keepdims=True)
            m_s[g] = m_next
            vg = vb_ref[pl.ds(k0, KT), g * LANES:(g + 1) * LANES]
            acc_s[g] = acc_s[g] * alpha + jnp.dot(p.astype(BF16), vg, preferred_element_type=F32)
        return eq_seen + jnp.sum(eqf, axis=0, keepdims=True)

    lax.fori_loop(0, ntile, att_body, jnp.zeros((1, QB), F32))

    for g in range(KV_HEADS):
        o = acc_s[g] / l_s[g]
        for r in range(N_HEADS // KV_HEADS):
            o_ref[:, (g * 4 + r) * LANES:(g * 4 + r + 1) * LANES] = o[r * QB:(r + 1) * QB, :].astype(BF16)


def _dsa_prompt(q_bf, qi, kiwi, tri, ki2, kb, vb):
    nb = SEQ // QB
    qrow = lambda w: pl.BlockSpec((QB, w), lambda b, n: (b * nb + n, 0))
    kfull = lambda w: pl.BlockSpec((SEQ, w), lambda b, n: (b, 0))
    return pl.pallas_call(
        _dsa_prompt_kernel,
        grid=(BATCH, nb),
        in_specs=[qrow(B_Q), qrow(IDX_HEADS * IDX_DIM), qrow(LANES),
                  pl.BlockSpec((KT, KT), lambda b, n: (0, 0)),
                  kfull(LANES), kfull(B_KV), kfull(B_KV)],
        out_specs=qrow(B_Q),
        out_shape=jax.ShapeDtypeStruct((NP_ROWS, B_Q), BF16),
        scratch_shapes=[
            pltpu.VMEM((SEQ, QB), I32),
            pltpu.VMEM((IDX_HEADS, QB, LANES), BF16),
            pltpu.VMEM((KV_HEADS, 4 * QB, HEAD_DIM), BF16),
            pltpu.VMEM((KV_HEADS, 4 * QB, LANES), F32),
            pltpu.VMEM((KV_HEADS, 4 * QB, LANES), F32),
            pltpu.VMEM((KV_HEADS, 4 * QB, HEAD_DIM), F32),
        ],
        compiler_params=_cparams(("arbitrary", "arbitrary"), 48),
        name="dsa_prompt",
    )(q_bf, qi, kiwi, tri, ki2, kb, vb)


PG = 8
NPC = N_PAGES // PG
PGK = PG * PAGE_SIZE
QPAD = 8
L_ALL = (N_PAGES + 1) * PAGE_SIZE


def _dsa_sample_kernel(pt_ref, qi_ref, wcol_ref, qs_ref, triu_ref, kin_ref, kn_ref, vn_ref, *rest):
    ik_refs = rest[:PG]
    k_refs = rest[PG:2 * PG]
    v_refs = rest[2 * PG:3 * PG]
    o_ref = rest[3 * PG]
    key_s, thr_s, need_s, eq_s, m_s, l_s, acc_s = rest[3 * PG + 1:]
    ph = pl.program_id(1)
    pc = pl.program_id(2)
    nrep = N_HEADS // KV_HEADS

    def tile_scores(ik_tile):
        d = lax.dot_general(qi_ref[0].astype(BF16), ik_tile.astype(BF16), (((1,), (1,)), ((), ())),
                            preferred_element_type=F32)
        sc = jnp.maximum(d * (IDX_DIM ** -0.5), 0.0) * wcol_ref[0][:, :1]
        return sc.reshape(QPAD, IDX_HEADS, ik_tile.shape[0]).sum(axis=1)

    def new_admissible():
        j = lax.broadcasted_iota(I32, (QPAD, PAGE_SIZE), 1)
        qq = lax.broadcasted_iota(I32, (QPAD, PAGE_SIZE), 0)
        return (j <= qq) & (j < DEC_SEQ)

    @pl.when(ph == 0)
    def _():
        ik_all = jnp.concatenate([ik_refs[g][0, 0] for g in range(PG)], axis=0)
        off = pl.multiple_of(pc * PGK, PGK)
        key_s[:, pl.ds(off, PGK)] = _sortable(tile_scores(ik_all))

        @pl.when(pc == NPC - 1)
        def _():
            sc = jnp.where(new_admissible(), tile_scores(kin_ref[0]), -jnp.inf)
            key_s[:, N_PAGES * PAGE_SIZE:] = _sortable(sc)

    def attend(kk, k_pages, v_pages, admissible):
        thr = thr_s[:, :1]
        need = need_s[:, :1]
        eqf = jnp.where(kk == thr, 1.0, 0.0)
        seen = eq_s[:, :1]
        ranks = []
        for c in range(len(k_pages)):
            e = eqf[:, c * PAGE_SIZE:(c + 1) * PAGE_SIZE]
            ranks.append(jnp.dot(e.astype(BF16), triu_ref[...], preferred_element_type=F32) + seen)
            seen = seen + jnp.sum(e, axis=1, keepdims=True)
        eq_s[...] = jnp.broadcast_to(seen, eq_s.shape)
        rank = jnp.concatenate(ranks, axis=1) if len(ranks) > 1 else ranks[0]
        tie = jnp.where(rank <= need, eqf, 0.0)
        sel = jnp.where(kk > thr, 1.0, tie)
        bias = (sel - 1.0) * (-MASK_NEG)
        if admissible is not None:
            bias = jnp.where(admissible, bias, MASK_NEG)
        bias_r = jnp.concatenate([bias] * nrep, axis=0)
        for g in range(KV_HEADS):
            kg = [kp[:, g, :].astype(BF16) for kp in k_pages]
            vg = [vp[:, g, :].astype(BF16) for vp in v_pages]
            kg = jnp.concatenate(kg, axis=0) if len(kg) > 1 else kg[0]
            vg = jnp.concatenate(vg, axis=0) if len(vg) > 1 else vg[0]
            s = lax.dot_general(qs_ref[0, g], kg, (((1,), (1,)), ((), ())), preferred_element_type=F32)
            s = s * (HEAD_DIM ** -0.5) + bias_r
            m_prev = m_s[g]
            m_next = jnp.maximum(m_prev, jnp.max(s, axis=1, keepdims=True))
            alpha = jnp.exp(m_prev - m_next)
            p = jnp.exp(s - m_next[:, :1])
            l_s[g] = alpha * l_s[g] + jnp.sum(p, axis=1, keepdims=True)
            m_s[g] = m_next
            acc_s[g] = acc_s[g] * alpha + jnp.dot(p.astype(BF16), vg, preferred_element_type=F32)

    @pl.when(ph == 1)
    def _():
        @pl.when(pc == 0)
        def _():
            def count(pred):
                hit = jnp.where(pred(key_s[...]), 1, 0).astype(I32)
                return jnp.sum(hit, axis=1, keepdims=True)

            def bit_body(it, thr):
                cand = thr + lax.shift_left(jnp.int32(1), 31 - it)
                return jnp.where(count(lambda kk: kk >= cand) >= INDEX_TOPK, cand, thr)

            thr = lax.fori_loop(0, 32, bit_body, jnp.full((QPAD, 1), INT_MIN, I32))
            need = (INDEX_TOPK - count(lambda kk: kk > thr)).astype(F32)
            thr_s[...] = jnp.broadcast_to(thr, thr_s.shape)
            need_s[...] = jnp.broadcast_to(need, need_s.shape)
            eq_s[...] = jnp.zeros(eq_s.shape, F32)
            m_s[...] = jnp.full(m_s.shape, MASK_NEG, F32)
            l_s[...] = jnp.zeros(l_s.shape, F32)
            acc_s[...] = jnp.zeros(acc_s.shape, F32)

        off = pl.multiple_of(pc * PGK, PGK)
        attend(key_s[:, pl.ds(off, PGK)], [r.at[0, 0] for r in k_refs], [r.at[0, 0] for r in v_refs], None)

        @pl.when(pc == NPC - 1)
        def _():
            attend(key_s[:, N_PAGES * PAGE_SIZE:], [kn_ref.at[0]], [vn_ref.at[0]], new_admissible())
            for g in range(KV_HEADS):
                o_ref[0, g] = acc_s[g] / l_s[g]


def _dsa_sample(page_table, qi_s, wcol, qs, triu, ki_new, k_new, v_new, pool_ik, pool_k, pool_v, l):
    def page_spec(tail, phase, hold):
        def make(g):
            def imap(b, ph, pc, pt):
                page = jnp.where(ph == phase, pc * PG + g, hold + g)
                return (l, pt[b * N_PAGES + page]) + (0,) * len(tail)
            return pl.BlockSpec((1, 1) + tail, imap)
        return [make(g) for g in range(PG)]

    per_b = lambda shape: pl.BlockSpec((1,) + shape, lambda b, ph, pc, pt: (b,) + (0,) * len(shape))
    kv_page = (PAGE_SIZE, KV_HEADS, HEAD_DIM)
    grid_spec = pltpu.PrefetchScalarGridSpec(
        num_scalar_prefetch=1,
        grid=(DEC_BATCH, 2, NPC),
        in_specs=[per_b((QPAD * IDX_HEADS, IDX_DIM)), per_b((QPAD * IDX_HEADS, LANES)),
                  per_b((KV_HEADS, 4 * QPAD, HEAD_DIM)),
                  pl.BlockSpec((PAGE_SIZE, PAGE_SIZE), lambda b, ph, pc, pt: (0, 0)),
                  per_b((PAGE_SIZE, IDX_DIM)), per_b(kv_page), per_b(kv_page)]
                 + page_spec((PAGE_SIZE, IDX_DIM), 0, N_PAGES - PG) + page_spec(kv_page, 1, 0) + page_spec(kv_page, 1, 0),
        out_specs=per_b((KV_HEADS, 4 * QPAD, HEAD_DIM)),
        scratch_shapes=[
            pltpu.VMEM((QPAD, L_ALL), I32),
            pltpu.VMEM((QPAD, LANES), I32),
            pltpu.VMEM((QPAD, LANES), F32),
            pltpu.VMEM((QPAD, LANES), F32),
            pltpu.VMEM((KV_HEADS, 4 * QPAD, LANES), F32),
            pltpu.VMEM((KV_HEADS, 4 * QPAD, LANES), F32),
            pltpu.VMEM((KV_HEADS, 4 * QPAD, HEAD_DIM), F32),
        ],
    )
    return pl.pallas_call(
        _dsa_sample_kernel,
        grid_spec=grid_spec,
        out_shape=jax.ShapeDtypeStruct((DEC_BATCH, KV_HEADS, 4 * QPAD, HEAD_DIM), F32),
        compiler_params=_cparams(("arbitrary", "arbitrary", "arbitrary"), 56),
        name="dsa_sample",
    )(page_table.reshape(-1), qi_s, wcol, qs, triu, ki_new, k_new, v_new,
      *([pool_ik] * PG), *([pool_k] * PG), *([pool_v] * PG))


def _seg64_sum(x):
    lo = lax.broadcasted_iota(I32, (x.shape[0], LANES), 1) < C_HEAD_DIM
    outs = []
    for p in range(x.shape[1] // LANES):
        xp = x[:, p * LANES:(p + 1) * LANES]
        s0 = jnp.sum(jnp.where(lo, xp, 0.0), axis=1, keepdims=True)
        s1 = jnp.sum(jnp.where(lo, 0.0, xp), axis=1, keepdims=True)
        outs.append(jnp.where(lo, s0, s1))
    return jnp.concatenate(outs, axis=1)


def _dot_f32(a, b):
    return jnp.dot(a, b, preferred_element_type=F32, precision=lax.Precision.HIGHEST)


def _rwkv_prep_kernel(cx_ref, prev_ref, mu_ref, w0_ref, a0_ref, kk_ref, ka_ref, rk_ref, wup_ref, aup_ref, gup_ref,
                      r_ref, d_ref, k_ref, v_ref, kap_ref, bb_ref, g_ref, bonus_ref):
    cx = cx_ref[...]
    xm = cx + (prev_ref[...] - cx) * mu_ref[...]
    r = xm[:, 0:C_WIDTH]
    k = xm[:, C_WIDTH:2 * C_WIDTH]
    v = xm[:, 2 * C_WIDTH:3 * C_WIDTH]
    wa = xm[:, 3 * C_WIDTH:3 * C_WIDTH + LANES]
    gd = xm[:, 3 * C_WIDTH + LANES:]
    z = -(w0_ref[...] + _dot_f32(jnp.tanh(wa), wup_ref[...]))
    softplus = jnp.maximum(z, 0.0) + jnp.log1p(jnp.exp(-jnp.abs(z)))
    w = -softplus - 0.5
    a = _sigmoid(a0_ref[...] + _dot_f32(wa, aup_ref[...]))
    g = _dot_f32(_sigmoid(gd), gup_ref[...])
    kk = k * kk_ref[...]
    k2 = k * (1.0 + (a - 1.0) * ka_ref[...])
    kap = kk * lax.rsqrt(jnp.maximum(_seg64_sum(kk * kk), 1e-24))
    r_ref[...] = r
    d_ref[...] = jnp.exp(-jnp.exp(w))
    k_ref[...] = k2
    v_ref[...] = v
    kap_ref[...] = kap
    bb_ref[...] = kap * a
    g_ref[...] = g
    bonus_ref[...] = _seg64_sum(r * k2 * rk_ref[...]) * v


def _rwkv_prep(cx, prev, mu, w0, a0, k_k, k_a, r_k, wup_pad, aup_pad, g_up):
    nt = cx.shape[0]
    tm = ROW_TILE
    row = lambda w: pl.BlockSpec((tm, w), lambda i: (i, 0))
    const = lambda s: pl.BlockSpec(s, lambda i: (0, 0))
    return pl.pallas_call(
        _rwkv_prep_kernel,
        grid=(nt // tm,),
        in_specs=[row(C_COLS), row(C_COLS), const((1, C_COLS))] + [const((1, C_WIDTH))] * 5
                 + [const((LANES, C_WIDTH))] * 3,
        out_specs=[row(C_WIDTH)] * 8,
        out_shape=[jax.ShapeDtypeStruct((nt, C_WIDTH), F32)] * 8,
        compiler_params=_cparams(("arbitrary",), 48),
        name="rwkv_prep",
    )(cx, prev, mu, w0, a0, k_k, k_a, r_k, wup_pad, aup_pad, g_up)


SCAN_CHUNK = 64
PAIRS = C_HEADS // 2
V_KAP, V_DEC, V_BB, V_K, V_R = range(5)


def _split_bf16(x):
    hi = x.astype(BF16)
    return hi, (x - hi.astype(F32)).astype(BF16)


def _scan_steps(n_steps, n_pairs, row, s_ref, vth_ref, vtl_ref, yacc_ref, ones_ref, wt_ref, wy_ref, unroll):
    dot = functools.partial(jnp.dot, preferred_element_type=F32)

    def step(t, carry):
        wt = wt_ref[t]
        vcol = dot(vth_ref[...], wt) + dot(vtl_ref[...], wt)
        hi, lo = _split_bf16(jnp.concatenate([s_ref[p] * row(V_KAP, p, t) for p in range(n_pairs)], axis=0))
        sa = dot(hi, ones_ref[...]) + dot(lo, ones_ref[...])
        qs = []
        for p in range(n_pairs):
            sl = slice(p * C_HEAD_DIM, (p + 1) * C_HEAD_DIM)
            sn = s_ref[p] * row(V_DEC, p, t) - sa[sl] * row(V_BB, p, t) + vcol[sl] * row(V_K, p, t)
            s_ref[p] = sn
            qs.append(sn * row(V_R, p, t))
        yacc_ref[...] += dot(jnp.concatenate(qs, axis=0).astype(BF16), wy_ref[t])
        return carry

    lax.fori_loop(0, n_steps, step, 0, unroll=unroll)


def _group_norm_t(y):
    mu = jnp.mean(y, axis=0, keepdims=True)
    yc = y - mu
    var = jnp.mean(yc * yc, axis=0, keepdims=True)
    return yc * lax.rsqrt(var + C_GN_EPS)


def _scan_prompt_kernel(*refs):
    ins = refs[:12]
    ones_ref, wt_ref, wy_ref, y_ref, sout_ref, vth_s, vtl_s, yacc_s, xs_s = refs[12:]
    c = pl.program_id(0)
    lo_half = lax.broadcasted_iota(I32, (SCAN_CHUNK, LANES), 1) < C_HEAD_DIM
    zeros = jnp.zeros((SCAN_CHUNK, LANES), F32)

    @pl.when(c == 0)
    def _():
        sout_ref[...] = jnp.zeros(sout_ref.shape, F32)

    for b in range(BATCH):
        for p in range(PAIRS):
            vp = ins[b * 6 + 5][:, p * LANES:(p + 1) * LANES]
            vt = jnp.transpose(jnp.concatenate([vp, zeros], axis=0))
            vt2 = jnp.where(lo_half, vt[:C_HEAD_DIM], pltpu.roll(vt[C_HEAD_DIM:], 64, 1))
            hi, lo = _split_bf16(vt2)
            rows = slice((b * PAIRS + p) * C_HEAD_DIM, (b * PAIRS + p + 1) * C_HEAD_DIM)
            vth_s[rows, :] = hi
            vtl_s[rows, :] = lo
            for v in range(5):
                xs_s[v, b * PAIRS + p] = ins[b * 6 + v][:, p * LANES:(p + 1) * LANES]
    yacc_s[...] = jnp.zeros(yacc_s.shape, F32)

    row = lambda v, g, t: xs_s[v, g, pl.ds(t, 1), :]

    _scan_steps(SCAN_CHUNK, BATCH * PAIRS, row, sout_ref, vth_s, vtl_s, yacc_s, ones_ref, wt_ref, wy_ref, 2)

    for b in range(BATCH):
        for p in range(PAIRS):
            g = b * PAIRS + p
            yn = _group_norm_t(yacc_s[g * C_HEAD_DIM:(g + 1) * C_HEAD_DIM, :])
            yt = jnp.transpose(jnp.concatenate([yn, zeros], axis=0))
            y_ref[b, :, p * LANES:(p + 1) * LANES] = jnp.where(
                lo_half, yt[:C_HEAD_DIM], pltpu.roll(yt[C_HEAD_DIM:], 64, 1))


def _scan_prompt(vecs, ones_bd, wt, wy):
    nchunk = SEQ // SCAN_CHUNK
    specs, args = [], []
    for b in range(BATCH):
        for v in vecs:
            specs.append(pl.BlockSpec((SCAN_CHUNK, C_WIDTH), functools.partial(lambda c, b: (b * nchunk + c, 0), b=b)))
            args.append(v)
    g = BATCH * PAIRS
    full = lambda a: pl.BlockSpec(a.shape, lambda c: (0,) * a.ndim)
    return pl.pallas_call(
        _scan_prompt_kernel,
        grid=(nchunk,),
        in_specs=specs + [full(ones_bd), full(wt), full(wy)],
        out_specs=[pl.BlockSpec((BATCH, SCAN_CHUNK, C_WIDTH), lambda c: (0, c, 0)),
                   pl.BlockSpec((g, C_HEAD_DIM, LANES), lambda c: (0, 0, 0))],
        out_shape=[jax.ShapeDtypeStruct((BATCH, SEQ, C_WIDTH), F32), jax.ShapeDtypeStruct((g, C_HEAD_DIM, LANES), F32)],
        scratch_shapes=[pltpu.VMEM((g * C_HEAD_DIM, LANES), BF16), pltpu.VMEM((g * C_HEAD_DIM, LANES), BF16),
                        pltpu.VMEM((g * C_HEAD_DIM, LANES), F32), pltpu.VMEM((5, g, SCAN_CHUNK, LANES), F32)],
        compiler_params=_cparams(("arbitrary",), 48),
        name="scan_prompt",
    )(*args, ones_bd, wt, wy)


def _scan_sample_kernel(xs_ref, vt_ref, s0_ref, ones_ref, wt_ref, wy_ref, sout_ref, yn_ref, vth_s, vtl_s):
    n_pairs = DEC_BATCH * PAIRS
    sout_ref[...] = s0_ref[...]
    hi, lo = _split_bf16(vt_ref[...])
    vth_s[...] = hi
    vtl_s[...] = lo
    yn_ref[...] = jnp.zeros(yn_ref.shape, F32)
    row = lambda v, p, t: xs_ref[v, p, pl.ds(t, 1), :]
    _scan_steps(DEC_SEQ, n_pairs, row, sout_ref, vth_s, vtl_s, yn_ref, ones_ref, wt_ref, wy_ref, 1)
    for p in range(n_pairs):
        rows = slice(p * C_HEAD_DIM, (p + 1) * C_HEAD_DIM)
        yn_ref[rows, :] = _group_norm_t(yn_ref[rows, :])


def _scan_sample(xs, vt, s0, ones_bd, wt, wy):
    full = lambda a: pl.BlockSpec(a.shape, lambda i: (0,) * a.ndim)
    return pl.pallas_call(
        _scan_sample_kernel,
        grid=(1,),
        in_specs=[full(xs), full(vt), full(s0), full(ones_bd), full(wt), full(wy)],
        out_specs=[full(s0), full(vt)],
        out_shape=[jax.ShapeDtypeStruct(s0.shape, F32), jax.ShapeDtypeStruct(vt.shape, F32)],
        scratch_shapes=[pltpu.VMEM(vt.shape, BF16), pltpu.VMEM(vt.shape, BF16)],
        compiler_params=_cparams(("arbitrary",), 48),
        name="scan_sample",
    )(xs, vt, s0, ones_bd, wt, wy)


def _rwkv_out_kernel(y_ref, bonus_ref, g_ref, lg_ref, lb_ref, w_ref, o_ref):
    pre = (y_ref[...] * lg_ref[...] + lb_ref[...] + bonus_ref[...]) * g_ref[...]
    o_ref[...] = jnp.dot(pre.astype(BF16), w_ref[0], preferred_element_type=F32)


def _rwkv_out(y, bonus, g, ln_g, ln_b, w_bf, l):
    nt = y.shape[0]
    tm = ROW_TILE
    row = lambda w: pl.BlockSpec((tm, w), lambda i: (i, 0))
    const = lambda s: pl.BlockSpec(s, lambda i: (0, 0))
    return pl.pallas_call(
        _rwkv_out_kernel,
        grid=(nt // tm,),
        in_specs=[row(C_WIDTH)] * 3 + [const((1, C_WIDTH))] * 2 + [pl.BlockSpec((1, C_WIDTH, D_MODEL), lambda i: (l, 0, 0))],
        out_specs=row(D_MODEL),
        out_shape=jax.ShapeDtypeStruct((nt, D_MODEL), F32),
        compiler_params=_cparams(("arbitrary",), 48),
        name="rwkv_out",
    )(y, bonus, g, ln_g, ln_b, w_bf)


def _merge_kernel(a_ref, b_ref, c_ref, g0_ref, g1_ref, g2_ref, x_ref, w_ref, lg_ref, lb_ref, o_ref, obf_ref):
    merged = g0_ref[...] * a_ref[...] + g1_ref[...] * b_ref[...] + g2_ref[...] * c_ref[...]
    mix = jnp.dot(merged.astype(BF16), w_ref[0], preferred_element_type=F32)
    x1 = _layer_norm(DEEPNORM_ALPHA * x_ref[...] + mix, lg_ref[...], lb_ref[...], LN_EPS)
    o_ref[...] = x1
    obf_ref[...] = x1.astype(BF16)


def _merge(a_out, b_out, c_out, gates, x, wo_bf, ln_g, ln_b, l):
    nt = x.shape[0]
    tm = 128
    row = lambda j: pl.BlockSpec((tm, D_MODEL), lambda i: (i, j))
    const = lambda s: pl.BlockSpec(s, lambda i: (0, 0))
    return pl.pallas_call(
        _merge_kernel,
        grid=(nt // tm,),
        in_specs=[row(0), row(0), row(0), row(0), row(1), row(2), row(0),
                  pl.BlockSpec((1, D_MODEL, D_MODEL), lambda i: (l, 0, 0)),
                  const((1, D_MODEL)), const((1, D_MODEL))],
        out_specs=[row(0), row(0)],
        out_shape=[jax.ShapeDtypeStruct((nt, D_MODEL), F32), jax.ShapeDtypeStruct((nt, D_MODEL), BF16)],
        compiler_params=_cparams(("arbitrary",), 48),
        name="merge",
    )(a_out, b_out, c_out, gates, gates, gates, x, wo_bf, ln_g, ln_b)


def _router_kernel(x_ref, w_ref, b_ref, idx_ref, gate_ref):
    logits = _dot_f32(x_ref[...], w_ref[...]) + b_ref[...]
    lane = lax.broadcasted_iota(I32, logits.shape, 1)
    vals, idxs = [], []
    for _ in range(TOP_K):
        m = jnp.max(logits, axis=1, keepdims=True)
        idx = jnp.min(jnp.where(logits == m, lane, LANES), axis=1, keepdims=True)
        vals.append(m)
        idxs.append(idx)
        logits = jnp.where(lane == idx, -jnp.inf, logits)
    es = [jnp.exp(v - vals[0]) for v in vals]
    den = es[0] + es[1] + es[2] + es[3]
    idx_out = jnp.zeros(logits.shape, I32)
    gate_out = jnp.zeros(logits.shape, F32)
    for k in range(TOP_K):
        idx_out = jnp.where(lane == k, idxs[k], idx_out)
        gate_out = jnp.where(lane == k, es[k] / den, gate_out)
    idx_ref[...] = idx_out
    gate_ref[...] = gate_out


def _router(x1, rw_pad, rb_pad):
    nt = x1.shape[0]
    tm = ROW_TILE
    return pl.pallas_call(
        _router_kernel,
        grid=(nt // tm,),
        in_specs=[pl.BlockSpec((tm, D_MODEL), lambda i: (i, 0)), pl.BlockSpec((D_MODEL, LANES), lambda i: (0, 0)),
                  pl.BlockSpec((1, LANES), lambda i: (0, 0))],
        out_specs=[pl.BlockSpec((tm, LANES), lambda i: (i, 0))] * 2,
        out_shape=[jax.ShapeDtypeStruct((nt, LANES), I32), jax.ShapeDtypeStruct((nt, LANES), F32)],
        compiler_params=_cparams(("arbitrary",), 48),
        name="router",
    )(x1, rw_pad, rb_pad)


MOE_ROWS = 256
MOE_SLOTS = NT * TOP_K
MOE_BLOCKS = MOE_SLOTS // MOE_ROWS + N_EXPERTS
MOE_P = MOE_BLOCKS * MOE_ROWS
FF_TILE = 512
DN_TILE = 1024


def _moe_gather_kernel(tok_ref, nused_ref, x_hbm, o_ref, buf, sem):
    i = pl.program_id(0)
    n_used = nused_ref[0]
    slot = i % 2

    def row_copy(blk, s, r):
        tok = tok_ref[blk * MOE_ROWS + r]
        return pltpu.make_async_copy(x_hbm.at[pl.ds(tok, 1)], buf.at[s, pl.ds(r, 1)], sem.at[s])

    def start_block(blk, s):
        def body(r, c):
            row_copy(blk, s, r).start()
            return c
        lax.fori_loop(0, MOE_ROWS, body, 0, unroll=8)

    @pl.when((i == 0) & (n_used > 0))
    def _():
        start_block(0, 0)

    @pl.when(i + 1 < n_used)
    def _():
        start_block(i + 1, 1 - slot)

    @pl.when(i < n_used)
    def _():
        def body(r, c):
            row_copy(i, slot, r).wait()
            return c
        lax.fori_loop(0, MOE_ROWS, body, 0, unroll=8)
        o_ref[...] = buf[slot].astype(BF16)

    @pl.when(i >= n_used)
    def _():
        o_ref[...] = jnp.zeros(o_ref.shape, BF16)


def _moe_gather(row_tok, n_used, x1):
    grid_spec = pltpu.PrefetchScalarGridSpec(
        num_scalar_prefetch=2,
        grid=(MOE_BLOCKS,),
        in_specs=[pl.BlockSpec(memory_space=pl.ANY)],
        out_specs=pl.BlockSpec((MOE_ROWS, D_MODEL), lambda i, tok, nu: (i, 0)),
        scratch_shapes=[pltpu.VMEM((2, MOE_ROWS, D_MODEL), F32), pltpu.SemaphoreType.DMA((2,))],
    )
    return pl.pallas_call(
        _moe_gather_kernel,
        grid_spec=grid_spec,
        out_shape=jax.ShapeDtypeStruct((MOE_P, D_MODEL), BF16),
        compiler_params=_cparams(("arbitrary",), 32, disable_bounds_checks=True),
        name="moe_gather",
    )(row_tok, n_used, x1)


def _expert_changed(be_ref, i):
    return (i == 0) | (be_ref[i] != be_ref[jnp.maximum(i - 1, 0)])


def _gmm1_kernel(be_ref, nused_ref, x_ref, wg_ref, wl_ref, bg_ref, bl_ref, o_ref, wg_bf, wl_bf):
    i = pl.program_id(1)

    @pl.when(_expert_changed(be_ref, i))
    def _():
        wg_bf[...] = wg_ref[0, 0].astype(BF16)
        wl_bf[...] = wl_ref[0, 0].astype(BF16)

    @pl.when(i < nused_ref[0])
    def _():
        x = x_ref[...]
        glu = jnp.dot(x, wg_bf[...], preferred_element_type=F32) + bg_ref[0, 0]
        lin = jnp.dot(x, wl_bf[...], preferred_element_type=F32) + bl_ref[0, 0]
        glu = jnp.minimum(glu, SWIGLU_LIMIT)
        lin = jnp.clip(lin, -SWIGLU_LIMIT, SWIGLU_LIMIT)
        o_ref[...] = ((lin + 1.0) * glu * _sigmoid(SWIGLU_ALPHA * glu)).astype(BF16)

    @pl.when(i >= nused_ref[0])
    def _():
        o_ref[...] = jnp.zeros(o_ref.shape, BF16)


def _gmm1(block_exp, n_used, xs, w_gu, b_gu4, l):
    nff = EXPERT_FF // FF_TILE
    grid_spec = pltpu.PrefetchScalarGridSpec(
        num_scalar_prefetch=2,
        grid=(nff, MOE_BLOCKS),
        in_specs=[
            pl.BlockSpec((MOE_ROWS, D_MODEL), lambda j, i, be, nu: (i, 0)),
            pl.BlockSpec((1, 1, D_MODEL, FF_TILE), lambda j, i, be, nu: (l, be[i], 0, j)),
            pl.BlockSpec((1, 1, D_MODEL, FF_TILE), lambda j, i, be, nu: (l, be[i], 0, nff + j)),
            pl.BlockSpec((1, 1, 1, FF_TILE), lambda j, i, be, nu: (l, be[i], 0, j)),
            pl.BlockSpec((1, 1, 1, FF_TILE), lambda j, i, be, nu: (l, be[i], 0, nff + j)),
        ],
        out_specs=pl.BlockSpec((MOE_ROWS, FF_TILE), lambda j, i, be, nu: (i, j)),
        scratch_shapes=[pltpu.VMEM((D_MODEL, FF_TILE), BF16), pltpu.VMEM((D_MODEL, FF_TILE), BF16)],
    )
    return pl.pallas_call(
        _gmm1_kernel,
        grid_spec=grid_spec,
        out_shape=jax.ShapeDtypeStruct((MOE_P, EXPERT_FF), BF16),
        compiler_params=_cparams(("arbitrary", "arbitrary"), 48),
        name="moe_gate_up",
    )(block_exp, n_used, xs, w_gu, w_gu, b_gu4, b_gu4)


def _gmm2_kernel(be_ref, nused_ref, h_ref, w_ref, b_ref, o_ref, w_bf):
    i = pl.program_id(1)

    @pl.when(_expert_changed(be_ref, i))
    def _():
        w_bf[...] = w_ref[0, 0].astype(BF16)

    @pl.when(i < nused_ref[0])
    def _():
        o_ref[...] = jnp.dot(h_ref[...], w_bf[...], preferred_element_type=F32) + b_ref[0, 0]

    @pl.when(i >= nused_ref[0])
    def _():
        o_ref[...] = jnp.zeros(o_ref.shape, F32)


def _gmm2(block_exp, n_used, h, w_dn, b_dn4, l):
    grid_spec = pltpu.PrefetchScalarGridSpec(
        num_scalar_prefetch=2,
        grid=(D_MODEL // DN_TILE, MOE_BLOCKS),
        in_specs=[
            pl.BlockSpec((MOE_ROWS, EXPERT_FF), lambda j, i, be, nu: (i, 0)),
            pl.BlockSpec((1, 1, EXPERT_FF, DN_TILE), lambda j, i, be, nu: (l, be[i], 0, j)),
            pl.BlockSpec((1, 1, 1, DN_TILE), lambda j, i, be, nu: (l, be[i], 0, j)),
        ],
        out_specs=pl.BlockSpec((MOE_ROWS, DN_TILE), lambda j, i, be, nu: (i, j)),
        scratch_shapes=[pltpu.VMEM((EXPERT_FF, DN_TILE), BF16)],
    )
    return pl.pallas_call(
        _gmm2_kernel,
        grid_spec=grid_spec,
        out_shape=jax.ShapeDtypeStruct((MOE_P, D_MODEL), F32),
        compiler_params=_cparams(("arbitrary", "arbitrary"), 48),
        name="moe_down",
    )(block_exp, n_used, h, w_dn, b_dn4)


CMB_ROWS = 128


def _combine_kernel(dest_ref, ys_hbm, gate_ref, x_ref, lg_ref, lb_ref, o_ref, obf_ref, buf, sem):
    i = pl.program_id(0)
    slot = i % 2

    def row_copy(blk, s, t, k):
        src = dest_ref[(blk * CMB_ROWS + t) * TOP_K + k]
        return pltpu.make_async_copy(ys_hbm.at[pl.ds(src, 1)], buf.at[s, k, pl.ds(t, 1)], sem.at[s])

    def start_block(blk, s):
        def body(t, c):
            for k in range(TOP_K):
                row_copy(blk, s, t, k).start()
            return c
        lax.fori_loop(0, CMB_ROWS, body, 0, unroll=2)

    @pl.when(i == 0)
    def _():
        start_block(0, 0)

    @pl.when(i + 1 < pl.num_programs(0))
    def _():
        start_block(i + 1, 1 - slot)

    def wait_body(t, c):
        for k in range(TOP_K):
            row_copy(i, slot, t, k).wait()
        return c

    lax.fori_loop(0, CMB_ROWS, wait_body, 0, unroll=2)
    moe = jnp.zeros((CMB_ROWS, D_MODEL), F32)
    for k in range(TOP_K):
        moe = moe + gate_ref[:, k:k + 1] * buf[slot, k]
    x2 = _layer_norm(DEEPNORM_ALPHA * x_ref[...] + moe, lg_ref[...], lb_ref[...], LN_EPS)
    o_ref[...] = x2
    obf_ref[...] = x2.astype(BF16)


def _combine(dest, ys, gate, x1, ln_g, ln_b):
    nt = x1.shape[0]
    row = lambda w: pl.BlockSpec((CMB_ROWS, w), lambda i, d: (i, 0))
    const = lambda s: pl.BlockSpec(s, lambda i, d: (0, 0))
    grid_spec = pltpu.PrefetchScalarGridSpec(
        num_scalar_prefetch=1,
        grid=(nt // CMB_ROWS,),
        in_specs=[pl.BlockSpec(memory_space=pl.ANY), row(LANES), row(D_MODEL), const((1, D_MODEL)), const((1, D_MODEL))],
        out_specs=[row(D_MODEL), row(D_MODEL)],
        scratch_shapes=[pltpu.VMEM((2, TOP_K, CMB_ROWS, D_MODEL), F32), pltpu.SemaphoreType.DMA((2,))],
    )
    return pl.pallas_call(
        _combine_kernel,
        grid_spec=grid_spec,
        out_shape=[jax.ShapeDtypeStruct((nt, D_MODEL), F32), jax.ShapeDtypeStruct((nt, D_MODEL), BF16)],
        compiler_params=_cparams(("arbitrary",), 32, disable_bounds_checks=True),
        name="moe_combine",
    )(dest, ys, gate, x1, ln_g, ln_b)


def _rope_tables():
    pos = jnp.concatenate([jnp.tile(jnp.arange(SEQ, dtype=I32), BATCH),
                           PAST_LEN + jnp.tile(jnp.arange(DEC_SEQ, dtype=I32), DEC_BATCH),
                           jnp.zeros((NT - NP_ROWS - NS_ROWS,), I32)])

    def cos_sin(half):
        inv = ROPE_THETA ** (-jnp.arange(half, dtype=F32) / half)
        ang = pos.astype(F32)[:, None] * inv[None, :]
        return jnp.cos(ang), jnp.sin(ang)

    c, s = cos_sin(HEAD_DIM // 2)
    c128 = jnp.concatenate([c, c], axis=1)
    s128 = jnp.concatenate([-s, s], axis=1)
    c, s = cos_sin(IDX_DIM // 2)
    z = jnp.zeros_like(s)
    c64 = jnp.concatenate([c, c, c, c], axis=1)
    s64a = jnp.concatenate([-s, z, -s, z], axis=1)
    s64b = jnp.concatenate([z, s, z, s], axis=1)
    return c128, s128, c64, s64a, s64b


def _scan_tables():
    a = lax.broadcasted_iota(I32, (LANES, LANES), 0)
    b = lax.broadcasted_iota(I32, (LANES, LANES), 1)
    same_head = (a // C_HEAD_DIM) == (b // C_HEAD_DIM)
    t = jnp.arange(SCAN_CHUNK, dtype=I32)[:, None, None]
    ones_bd = same_head.astype(BF16)
    wt = (same_head[None] & ((a % C_HEAD_DIM)[None] == t)).astype(BF16)
    wy = (same_head[None] & ((b % C_HEAD_DIM)[None] == t)).astype(BF16)
    return ones_bd, wt, wy


def _pad_rows(x, rows):
    return jnp.concatenate([x, jnp.zeros((rows - x.shape[0],) + x.shape[1:], x.dtype)], axis=0)


def _sample_rows(x):
    return x[NP_ROWS:NP_ROWS + NS_ROWS]


def _spatial_weights(sgu_w, sgu_b):
    w_prompt = jnp.tril(sgu_w)
    w4 = jnp.tril(sgu_w[:, :DEC_SEQ, :DEC_SEQ])
    eye = jnp.eye(DEC_BATCH, dtype=F32)
    w_blk = jnp.einsum('ab,gij->gaibj', eye, w4).reshape(A_GROUPS, NS_ROWS, NS_ROWS)
    w_sample = jnp.zeros((A_GROUPS, CHUNK, CHUNK), F32).at[:, :NS_ROWS, :NS_ROWS].set(w_blk)
    b_prompt = jnp.repeat(sgu_b.T, CHUNK, axis=1)
    b4 = jnp.repeat(sgu_b[:, :DEC_SEQ].T, CHUNK, axis=1)
    b_sample = _pad_rows(jnp.tile(b4, (DEC_BATCH, 1)), CHUNK)
    return jnp.stack([w_prompt, w_sample]), jnp.stack([b_prompt, b_sample])


def _moe_routing(top_idx):
    e_flat = top_idx.reshape(MOE_SLOTS)
    onehot = (e_flat[:, None] == jnp.arange(N_EXPERTS, dtype=I32)[None, :]).astype(I32)
    csum = jnp.cumsum(onehot, axis=0)
    rank = jnp.sum(onehot * csum, axis=1) - 1
    counts = csum[-1]
    padded = (counts + MOE_ROWS - 1) // MOE_ROWS * MOE_ROWS
    seg_end = jnp.cumsum(padded)
    seg_start = seg_end - padded
    dest = (seg_start[e_flat] + rank).astype(I32)
    row_tok = jnp.zeros((MOE_P,), I32).at[dest].set(jnp.arange(MOE_SLOTS, dtype=I32) // TOP_K)
    block_row0 = jnp.arange(MOE_BLOCKS, dtype=I32) * MOE_ROWS
    block_exp = jnp.minimum(jnp.sum((seg_end[None, :] <= block_row0[:, None]).astype(I32), axis=1), N_EXPERTS - 1)
    n_used = (seg_end[-1] // MOE_ROWS).astype(I32).reshape(1)
    return row_tok, dest, block_exp.astype(I32), n_used


def _pair_tiles(t):
    b, _, r, n = t.shape
    t = t.reshape(b, PAIRS, 2, r, n)
    return jnp.transpose(t, (0, 1, 3, 2, 4)).reshape(b * PAIRS, r, 2 * n)


def _unpair_tiles(t, b):
    n = t.shape[-1] // 2
    t = t.reshape(b, PAIRS, t.shape[1], 2, n)
    return jnp.transpose(t, (0, 1, 3, 2, 4)).reshape(b, C_HEADS, t.shape[2], n)


def _layer(l, x, x_bf, C, P, shift_prev_s, state0_s, page_table):
    o_b = A_COLS
    uv = _mm(x_bf, P['w_in'], l, n=A_COLS, tm=768, tn=512, act=_gelu, name="proj_a")
    qkv = _mm(x_bf, P['w_in'], l, n=QKV_W, col0=o_b, tm=768, tn=512, name="proj_b")
    cx = _mm(x_bf, C['w_c'], l, n=C_COLS, tm=2112, tn=256, name="proj_c")
    gates = _mm(x_bf, C['w_g'], l, n=G_COLS, tm=768, tn=768, act=_sigmoid, name="proj_g")

    ws2, bias2 = _spatial_weights(P['sgu_w'][l], P['sgu_b'][l])
    a_out, v_tile = _gmlp(uv, P['ln_v_g'][l][None], P['ln_v_b'][l][None], ws2, bias2, C['w_a_out'], l)
    gmlp_v_s = v_tile[:NS_ROWS].reshape(DEC_BATCH, DEC_SEQ, A_WIDTH)

    q_bf, k_f, kb, vb, qi, kiwi, ki2 = _rope(qkv, C['rope'])
    v_f = qkv[:, B_Q + B_KV:B_Q + 2 * B_KV]
    b_att = _dsa_prompt(q_bf, qi, kiwi, C['tri'], ki2, kb, vb)

    pad_q = lambda t: jnp.pad(t, ((0, 0), (0, QPAD - DEC_SEQ)) + ((0, 0),) * (t.ndim - 2))
    pad_keys = lambda t: jnp.pad(t, ((0, 0), (0, PAGE_SIZE - DEC_SEQ)) + ((0, 0),) * (t.ndim - 2))
    qi_s = pad_q(_sample_rows(qi).reshape(DEC_BATCH, DEC_SEQ, IDX_HEADS * IDX_DIM)).reshape(
        DEC_BATCH, QPAD * IDX_HEADS, IDX_DIM)
    wi_s = pad_q(_sample_rows(kiwi)[:, IDX_DIM:IDX_DIM + IDX_HEADS].reshape(DEC_BATCH, DEC_SEQ, IDX_HEADS))
    wcol = jnp.broadcast_to((wi_s * IDX_HEADS ** -0.5).reshape(DEC_BATCH, QPAD * IDX_HEADS, 1),
                            (DEC_BATCH, QPAD * IDX_HEADS, LANES))
    nrep = N_HEADS // KV_HEADS
    qs = pad_q(_sample_rows(q_bf).reshape(DEC_BATCH, DEC_SEQ, KV_HEADS, nrep, HEAD_DIM))
    qs = jnp.transpose(qs, (0, 2, 3, 1, 4)).reshape(DEC_BATCH, KV_HEADS, nrep * QPAD, HEAD_DIM)
    ki_new = pad_keys(_sample_rows(kiwi)[:, :IDX_DIM].reshape(DEC_BATCH, DEC_SEQ, IDX_DIM))
    k_new = pad_keys(_sample_rows(k_f).reshape(DEC_BATCH, DEC_SEQ, KV_HEADS, HEAD_DIM))
    v_new = pad_keys(_sample_rows(v_f).reshape(DEC_BATCH, DEC_SEQ, KV_HEADS, HEAD_DIM))
    o_s = _dsa_sample(page_table, qi_s, wcol, qs, C['triu'], ki_new, k_new, v_new,
                      P['cache_idx_k'], P['cache_k'], P['cache_v'], l)
    o_s = o_s.reshape(DEC_BATCH, KV_HEADS, nrep, QPAD, HEAD_DIM)[:, :, :, :DEC_SEQ]
    o_s = jnp.transpose(o_s, (0, 3, 1, 2, 4)).reshape(NS_ROWS, B_Q).astype(BF16)
    b_att = jnp.concatenate([b_att, _pad_rows(o_s, ROW_TILE)], axis=0)
    b_out = _mm(b_att, P['w_b_out'], l, n=D_MODEL, tm=768, tn=512, name="proj_b_out")

    cx_p = cx[:NP_ROWS].reshape(BATCH, SEQ, C_COLS)
    cx_s = _sample_rows(cx).reshape(DEC_BATCH, DEC_SEQ, C_COLS)
    prev_p = jnp.concatenate([jnp.zeros((BATCH, 1, C_COLS), F32), cx_p[:, :-1]], axis=1).reshape(NP_ROWS, C_COLS)
    prev_s = jnp.concatenate([shift_prev_s[:, None, :], cx_s[:, :-1]], axis=1).reshape(NS_ROWS, C_COLS)
    prev = _pad_rows(jnp.concatenate([prev_p, prev_s], axis=0), NT)
    zpad = jnp.zeros((C_DECAY_RANK, C_WIDTH), F32)
    r, dec, k2, v, kap, bb, g, bonus = _rwkv_prep(
        cx, prev, P['mu_shift'][l][None], P['w0'][l][None], P['a0'][l][None], P['k_k'][l][None], P['k_a'][l][None],
        P['r_k'][l].reshape(1, C_WIDTH), jnp.concatenate([P['w_up'][l], zpad], axis=0),
        jnp.concatenate([zpad, P['a_up'][l]], axis=0), P['g_up'][l])
    ones_bd, wt, wy = C['scan']
    y_p, s_p = _scan_prompt((kap, dec, bb, k2, r, v), ones_bd, wt, wy)
    n_pairs = DEC_BATCH * PAIRS

    def step_rows(t):
        t = jnp.transpose(_sample_rows(t).reshape(DEC_BATCH, DEC_SEQ, PAIRS, LANES), (0, 2, 1, 3))
        return jnp.pad(t.reshape(n_pairs, DEC_SEQ, LANES), ((0, 0), (0, SUBLANES - DEC_SEQ), (0, 0)))

    xs_s = jnp.stack([step_rows(kap), step_rows(dec), step_rows(bb), step_rows(k2), step_rows(r)])
    vt_s = jnp.transpose(_sample_rows(v).reshape(DEC_BATCH, DEC_SEQ, C_HEADS, C_HEAD_DIM), (0, 2, 3, 1))
    vt_s = _pair_tiles(jnp.pad(vt_s, ((0, 0), (0, 0), (0, 0), (0, C_HEAD_DIM - DEC_SEQ))))
    s_s, yn_s = _scan_sample(xs_s, vt_s.reshape(n_pairs * C_HEAD_DIM, LANES), _pair_tiles(state0_s),
                             ones_bd, wt[:SUBLANES], wy[:SUBLANES])
    y_s = _unpair_tiles(yn_s.reshape(n_pairs, C_HEAD_DIM, LANES), DEC_BATCH)[..., :DEC_SEQ]
    y_s = jnp.transpose(y_s, (0, 3, 1, 2)).reshape(NS_ROWS, C_WIDTH)
    y_all = _pad_rows(jnp.concatenate([y_p.reshape(NP_ROWS, C_WIDTH), y_s], axis=0), NT)
    c_out = _rwkv_out(y_all, bonus, g, P['ln_x_g'][l][None], P['ln_x_b'][l][None], C['w_c_out'], l)

    x1, x1_bf = _merge(a_out, b_out, c_out, gates, x, C['w_o'], P['ln1_g'][l][None], P['ln1_b'][l][None], l)

    rw_pad = jnp.pad(P['router_w'][l], ((0, 0), (0, LANES - N_EXPERTS)))
    rb_pad = jnp.concatenate([P['router_b'][l], jnp.full((LANES - N_EXPERTS,), MASK_NEG, F32)])[None]
    idx128, gate128 = _router(x1, rw_pad, rb_pad)
    row_tok, dest, block_exp, n_used = _moe_routing(idx128[:, :TOP_K])
    xs = _moe_gather(row_tok, n_used, x1)
    h = _gmm1(block_exp, n_used, xs, P['w_gu'], C['b_gu'], l)
    ys = _gmm2(block_exp, n_used, h, P['w_dn'], C['b_dn'], l)
    x2, x2_bf = _combine(dest, ys, gate128, x1, P['ln2_g'][l][None], P['ln2_b'][l][None])

    new = dict(
        k_p=k_f[:NP_ROWS].reshape(BATCH, SEQ, KV_HEADS, HEAD_DIM),
        v_p=v_f[:NP_ROWS].reshape(BATCH, SEQ, KV_HEADS, HEAD_DIM),
        ik_p=kiwi[:NP_ROWS, :IDX_DIM].reshape(BATCH, SEQ, IDX_DIM),
        s_p=_unpair_tiles(s_p, BATCH),
        sh_p=cx_p[:, -1],
        k_s=_sample_rows(k_f).reshape(DEC_BATCH, DEC_SEQ, KV_HEADS, HEAD_DIM),
        v_s=_sample_rows(v_f).reshape(DEC_BATCH, DEC_SEQ, KV_HEADS, HEAD_DIM),
        ik_s=_sample_rows(kiwi)[:, :IDX_DIM].reshape(DEC_BATCH, DEC_SEQ, IDX_DIM),
        s_s=_unpair_tiles(s_s, DEC_BATCH),
        sh_s=cx_s[:, -1],
        gv_s=gmlp_v_s,
    )
    return x2, x2_bf, new


def kernel(x_prompt, x_sample, cache_k, cache_v, cache_idx_k, state_rwkv, state_shift, page_table, w_in, ln_v_g, ln_v_b, sgu_w, sgu_b, w_a_out, w_b_out, mu_shift, w0, w_up, a0, a_up, g_up, k_k, k_a, r_k, ln_x_g, ln_x_b, w_c_out, w_o, ln1_g, ln1_b, router_w, router_b, w_gu, b_gu, w_dn, b_dn, ln2_g, ln2_b):
    P = dict(cache_k=cache_k, cache_v=cache_v, cache_idx_k=cache_idx_k, w_in=w_in, ln_v_g=ln_v_g, ln_v_b=ln_v_b,
             sgu_w=sgu_w, sgu_b=sgu_b, w_b_out=w_b_out, mu_shift=mu_shift, w0=w0, w_up=w_up, a0=a0,
             a_up=a_up, g_up=g_up, k_k=k_k, k_a=k_a, r_k=r_k, ln_x_g=ln_x_g, ln_x_b=ln_x_b,
             ln1_g=ln1_g, ln1_b=ln1_b, router_w=router_w, router_b=router_b, w_gu=w_gu, w_dn=w_dn,
             ln2_g=ln2_g, ln2_b=ln2_b)
    ii = lax.broadcasted_iota(I32, (KT, KT), 0)
    jj = lax.broadcasted_iota(I32, (KT, KT), 1)
    o_c = A_COLS + B_COLS
    C = dict(
        rope=_rope_tables(),
        scan=_scan_tables(),
        tri=(jj <= ii).astype(BF16),
        triu=(ii[:PAGE_SIZE, :PAGE_SIZE] <= jj[:PAGE_SIZE, :PAGE_SIZE]).astype(BF16),
        w_c=w_in[:, :, o_c:o_c + C_COLS],
        w_g=w_in[:, :, o_c + C_COLS:],
        w_a_out=w_a_out.astype(BF16),
        w_c_out=w_c_out.astype(BF16),
        w_o=w_o.astype(BF16),
        b_gu=b_gu.reshape(DEPTH, N_EXPERTS, 1, 2 * EXPERT_FF),
        b_dn=b_dn.reshape(DEPTH, N_EXPERTS, 1, D_MODEL),
    )
    x = _pad_rows(jnp.concatenate([x_prompt.reshape(NP_ROWS, D_MODEL), x_sample.reshape(NS_ROWS, D_MODEL)], axis=0), NT)
    x_bf = x.astype(BF16)
    outs = []
    for l in range(DEPTH):
        x, x_bf, new = _layer(l, x, x_bf, C, P, state_shift[l], state_rwkv[l], page_table)
        outs.append(new)
    stack = lambda name: jnp.stack([o[name] for o in outs])
    y_prompt = x[:NP_ROWS].reshape(BATCH, SEQ, D_MODEL)
    y_sample = _sample_rows(x).reshape(DEC_BATCH, DEC_SEQ, D_MODEL)
    return (y_prompt, y_sample, stack('k_p'), stack('v_p'), stack('ik_p'), stack('s_p'), stack('sh_p'),
            stack('k_s'), stack('v_s'), stack('ik_s'), stack('s_s'), stack('sh_s'), stack('gv_s'))
```
